```python
import math
import jax, jax.numpy as jnp
from jax import lax
import numpy as np

D_MODEL = 2048
BATCH = 4
SEQ = 8192
DEPTH = 4
DEC_BATCH = 8
DEC_SEQ = 64
PAST_LEN = 4096

CHUNK = 64
N_MIXERS = 2
N_MLA = (DEPTH + 1) // 2
N_SSM = DEPTH // 2
N_HEADS = 16
QK_NOPE = 128
QK_ROPE = 64
QK_HEAD = QK_NOPE + QK_ROPE
V_HEAD = 128
Q_LORA = 512
KV_LORA = 256
ROPE_THETA = 10000.0
Q_BLOCK = 128
SSM_GROUP = 16
N_GROUPS = D_MODEL // SSM_GROUP
SSM_STATE = 64
SSM_BLOCK = CHUNK
DT_MIN = 0.001
DT_MAX = 0.1
D_FF = 5632
CONV_W = 3
EPS = 1e-6

kernel_name = 'hybrid_mla_s5_convffn_stream_step'


def rms_norm(x, g):
    xf = x.astype(jnp.float32)
    y = xf * lax.rsqrt(jnp.mean(xf * xf, axis=-1, keepdims=True) + EPS)
    return (y * g.astype(jnp.float32)).astype(x.dtype)


def modulate(x, shift, scale):
    return x * (1 + scale[:, None, :]) + shift[:, None, :]


def rope(x, pos):
    half = QK_ROPE // 2
    inv = ROPE_THETA ** (-jnp.arange(half, dtype=jnp.float32) / half)
    ang = pos.astype(jnp.float32)[:, None] * inv[None, :]
    cos = jnp.cos(ang)[None, :, None, :]
    sin = jnp.sin(ang)[None, :, None, :]
    xf = x.astype(jnp.float32)
    x1, x2 = xf[..., :half], xf[..., half:]
    return jnp.concatenate([x1 * cos - x2 * sin, x1 * sin + x2 * cos], axis=-1).astype(x.dtype)


def chunk_causal_attention(q, k, v, q_pos, k_pos):
    scale = QK_HEAD ** -0.5
    k_chunk = k_pos // CHUNK

    def block(args):
        qb, pb = args
        s = jnp.einsum('bqhe,bkhe->bhqk', qb, k, preferred_element_type=jnp.float32) * scale
        allowed = k_chunk[None, :] <= (pb // CHUNK)[:, None]
        s = jnp.where(allowed[None, None], s, -jnp.inf)
        p = jax.nn.softmax(s, axis=-1).astype(v.dtype)
        return jnp.einsum('bhqk,bkhd->bqhd', p, v)

    B, Lq = q.shape[:2]
    if Lq <= Q_BLOCK:
        return block((q, q_pos))
    nb = Lq // Q_BLOCK
    qb = q.reshape(B, nb, Q_BLOCK, N_HEADS, QK_HEAD).transpose(1, 0, 2, 3, 4)
    pb = q_pos.reshape(nb, Q_BLOCK)
    o = lax.map(block, (qb, pb))
    return o.transpose(1, 0, 2, 3, 4).reshape(B, Lq, N_HEADS, V_HEAD)


def mla_keys_values(ckv, krope, w_ukv, g_kn, pos):
    B, L, _ = ckv.shape
    kv = (ckv @ w_ukv).reshape(B, L, N_HEADS, QK_NOPE + V_HEAD)
    k_nope, v = kv[..., :QK_NOPE], kv[..., QK_NOPE:]
    k_rope = jnp.broadcast_to(krope[:, :, None, :], (B, L, N_HEADS, QK_ROPE)).astype(k_nope.dtype)
    k = rms_norm(jnp.concatenate([k_nope, k_rope], axis=-1), g_kn)
    k = jnp.concatenate([k[..., :QK_NOPE], rope(k[..., QK_NOPE:], pos)], axis=-1)
    return k, v


def mla_mixer(h, past, i, prm):
    B, L, _ = h.shape
    P = 0 if past is None else past[0].shape[1]
    q_pos = P + jnp.arange(L, dtype=jnp.int32)
    q = rms_norm(h @ prm['mla_w_dq'][i], prm['mla_g_qa'][i]) @ prm['mla_w_uq'][i]
    q = rms_norm(q.reshape(B, L, N_HEADS, QK_HEAD), prm['mla_g_qn'][i])
    q = jnp.concatenate([q[..., :QK_NOPE], rope(q[..., QK_NOPE:], q_pos)], axis=-1)
    kv_a = h @ prm['mla_w_dkv'][i]
    ckv = rms_norm(kv_a[..., :KV_LORA], prm['mla_g_kva'][i])
    krope = kv_a[..., KV_LORA:]
    if past is None:
        ckv_all, krope_all = ckv, krope
    else:
        ckv_all = jnp.concatenate([past[0].astype(ckv.dtype), ckv], axis=1)
        krope_all = jnp.concatenate([past[1].astype(krope.dtype), krope], axis=1)
    k_pos = jnp.arange(P + L, dtype=jnp.int32)
    k, v = mla_keys_values(ckv_all, krope_all, prm['mla_w_ukv'][i], prm['mla_g_kn'][i], k_pos)
    o = chunk_causal_attention(q, k, v, q_pos, k_pos)
    out = o.reshape(B, L, N_HEADS * V_HEAD) @ prm['mla_w_o'][i]
    return out, ckv, krope


def linear_combine(left, right):
    a1, b1 = left
    a2, b2 = right
    return a1 * a2, a2 * b1 + b2


def s5_scan(u, h0, A_bar, B_bar, C):
    Bsz, L = u.shape[:2]
    blk = math.gcd(L, SSM_BLOCK)
    nb = L // blk
    ub = u.reshape(Bsz, nb, blk, N_GROUPS, SSM_GROUP).transpose(1, 0, 2, 3, 4)
    a = jnp.broadcast_to(A_bar[None, None], (1, blk, N_GROUPS, SSM_STATE))

    def step(h, u_blk):
        bu = jnp.einsum('blgp,gnp->blgn', u_blk.astype(jnp.complex64), B_bar)
        a_cum, x = lax.associative_scan(linear_combine, (a, bu), axis=1)
        x = x + a_cum * h[:, None]
        y = jnp.einsum('blgn,gpn->blgp', x, C).real
        return x[:, -1], y

    h_last, y = lax.scan(step, h0, ub)
    y = y.transpose(1, 0, 2, 3, 4).reshape(Bsz, L, N_GROUPS * SSM_GROUP)
    return y, h_last


def s5_mixer(h, past, i, prm):
    B, L, _ = h.shape
    A = lax.complex(prm['s5_a_re'][i].astype(jnp.float32), prm['s5_a_im'][i].astype(jnp.float32))
    dt = jnp.exp(prm['s5_log_dt'][i].astype(jnp.float32))[:, None]
    A_bar = jnp.exp(A * dt)
    Bc = lax.complex(prm['s5_b_re'][i].astype(jnp.float32), prm['s5_b_im'][i].astype(jnp.float32))
    B_bar = ((A_bar - 1) / A)[..., None] * Bc
    C = lax.complex(prm['s5_c_re'][i].astype(jnp.float32), prm['s5_c_im'][i].astype(jnp.float32))
    if past is None:
        h0 = jnp.zeros((B, N_GROUPS, SSM_STATE), jnp.complex64)
    else:
        h0 = lax.complex(past[0].astype(jnp.float32), past[1].astype(jnp.float32))
    u = h.astype(jnp.float32)
    y, h_last = s5_scan(u.reshape(B, L, N_GROUPS, SSM_GROUP), h0, A_bar, B_bar, C)
    y = y + prm['s5_d'][i].astype(jnp.float32) * u
    z = jax.nn.gelu(y).astype(h.dtype)
    out = (z @ prm['s5_w_glu'][i]) * jax.nn.sigmoid(z @ prm['s5_w_gate'][i])
    return out, h_last.real.astype(h.dtype), h_last.imag.astype(h.dtype)


def conv_ffn(h, past, l, prm):
    B, L, _ = h.shape
    g = h @ prm['ffn_w_gate'][l]
    if past is None:
        past = jnp.zeros((B, CONV_W - 1, D_FF), g.dtype)
    gp = jnp.concatenate([past.astype(g.dtype), g], axis=1)
    w = prm['ffn_conv_w'][l]
    gc = prm['ffn_conv_b'][l] + w[0] * gp[:, 0:L]
    for k in range(1, CONV_W):
        gc = gc + w[k] * gp[:, k:k + L]
    out = (jax.nn.silu(gc) * (h @ prm['ffn_w_up'][l])) @ prm['ffn_w_down'][l]
    return out, gp[:, L:]


def run_layer(l, x, c, mix_past, conv_past, prm):
    mod = jax.nn.silu(c) @ prm['w_mod'][l] + prm['b_mod'][l]
    sh_m, sc_m, gt_m, sh_f, sc_f, gt_f = jnp.split(mod, 6, axis=-1)
    h = modulate(rms_norm(x, prm['g_norm_mix'][l]), sh_m, sc_m)
    i = l // N_MIXERS
    if l % N_MIXERS == 0:
        out, s1, s2 = mla_mixer(h, mix_past, i, prm)
    else:
        out, s1, s2 = s5_mixer(h, mix_past, i, prm)
    x = x + gt_m[:, None, :] * out
    h = modulate(rms_norm(x, prm['g_norm_ffn'][l]), sh_f, sc_f)
    out, conv_new = conv_ffn(h, conv_past, l, prm)
    x = x + gt_f[:, None, :] * out
    return x, s1, s2, conv_new


def setup_inputs(seed: int = 0) -> dict:
    key = jax.random.key(seed)
    ks = iter(jax.random.split(key, 48))

    def nrm(shape, scale):
        return jax.random.normal(next(ks), shape, jnp.float32) * scale

    def gain(shape):
        return 1.0 + nrm(shape, 0.02)

    n_idx = jnp.arange(SSM_STATE, dtype=jnp.float32)
    inp = {}
    inp['x_prompt'] = nrm((BATCH, SEQ, D_MODEL), 1.0)
    inp['x_sample'] = nrm((DEC_BATCH, DEC_SEQ, D_MODEL), 1.0)
    inp['cache_mla_ckv'] = nrm((N_MLA, DEC_BATCH, PAST_LEN, KV_LORA), 1.0)
    inp['cache_mla_krope'] = nrm((N_MLA, DEC_BATCH, PAST_LEN, QK_ROPE), 1.0)
    inp['state_s5_re'] = nrm((N_SSM, DEC_BATCH, N_GROUPS, SSM_STATE), 0.1)
    inp['state_s5_im'] = nrm((N_SSM, DEC_BATCH, N_GROUPS, SSM_STATE), 0.1)
    inp['cache_ffn_conv'] = nrm((DEPTH, DEC_BATCH, CONV_W - 1, D_FF), 1.0)
    inp['c_prompt'] = nrm((BATCH, D_MODEL), 1.0)
    inp['c_sample'] = nrm((DEC_BATCH, D_MODEL), 1.0)
    inp['w_mod'] = nrm((DEPTH, D_MODEL, 6 * D_MODEL), 0.5 * D_MODEL ** -0.5)
    inp['b_mod'] = nrm((DEPTH, 6 * D_MODEL), 0.01)
    inp['g_norm_mix'] = gain((DEPTH, D_MODEL))
    inp['g_norm_ffn'] = gain((DEPTH, D_MODEL))
    inp['mla_w_dq'] = nrm((N_MLA, D_MODEL, Q_LORA), D_MODEL ** -0.5)
    inp['mla_g_qa'] = gain((N_MLA, Q_LORA))
    inp['mla_w_uq'] = nrm((N_MLA, Q_LORA, N_HEADS * QK_HEAD), Q_LORA ** -0.5)
    inp['mla_g_qn'] = gain((N_MLA, QK_HEAD))
    inp['mla_w_dkv'] = nrm((N_MLA, D_MODEL, KV_LORA + QK_ROPE), D_MODEL ** -0.5)
    inp['mla_g_kva'] = gain((N_MLA, KV_LORA))
    inp['mla_w_ukv'] = nrm((N_MLA, KV_LORA, N_HEADS * (QK_NOPE + V_HEAD)), KV_LORA ** -0.5)
    inp['mla_g_kn'] = gain((N_MLA, QK_HEAD))
    inp['mla_w_o'] = nrm((N_MLA, N_HEADS * V_HEAD, D_MODEL), (N_HEADS * V_HEAD) ** -0.5)
    inp['s5_a_re'] = -0.5 - jnp.abs(nrm((N_SSM, N_GROUPS, SSM_STATE), 0.01))
    inp['s5_a_im'] = math.pi * n_idx + nrm((N_SSM, N_GROUPS, SSM_STATE), 0.01)
    inp['s5_log_dt'] = jax.random.uniform(next(ks), (N_SSM, N_GROUPS), jnp.float32, math.log(DT_MIN), math.log(DT_MAX))
    inp['s5_b_re'] = nrm((N_SSM, N_GROUPS, SSM_STATE, SSM_GROUP), (2 * SSM_GROUP) ** -0.5)
    inp['s5_b_im'] = nrm((N_SSM, N_GROUPS, SSM_STATE, SSM_GROUP), (2 * SSM_GROUP) ** -0.5)
    inp['s5_c_re'] = nrm((N_SSM, N_GROUPS, SSM_GROUP, SSM_STATE), (2 * SSM_STATE) ** -0.5)
    inp['s5_c_im'] = nrm((N_SSM, N_GROUPS, SSM_GROUP, SSM_STATE), (2 * SSM_STATE) ** -0.5)
    inp['s5_d'] = nrm((N_SSM, D_MODEL), 1.0)
    inp['s5_w_glu'] = nrm((N_SSM, D_MODEL, D_MODEL), D_MODEL ** -0.5)
    inp['s5_w_gate'] = nrm((N_SSM, D_MODEL, D_MODEL), D_MODEL ** -0.5)
    inp['ffn_w_gate'] = nrm((DEPTH, D_MODEL, D_FF), D_MODEL ** -0.5)
    inp['ffn_w_up'] = nrm((DEPTH, D_MODEL, D_FF), D_MODEL ** -0.5)
    inp['ffn_conv_w'] = nrm((DEPTH, CONV_W, D_FF), CONV_W ** -0.5)
    inp['ffn_conv_b'] = nrm((DEPTH, D_FF), 0.01)
    inp['ffn_w_down'] = nrm((DEPTH, D_FF, D_MODEL), D_FF ** -0.5)
    return inp


def reference(x_prompt, x_sample, cache_mla_ckv, cache_mla_krope, state_s5_re, state_s5_im, cache_ffn_conv, c_prompt, c_sample, w_mod, b_mod, g_norm_mix, g_norm_ffn, mla_w_dq, mla_g_qa, mla_w_uq, mla_g_qn, mla_w_dkv, mla_g_kva, mla_w_ukv, mla_g_kn, mla_w_o, s5_a_re, s5_a_im, s5_log_dt, s5_b_re, s5_b_im, s5_c_re, s5_c_im, s5_d, s5_w_glu, s5_w_gate, ffn_w_gate, ffn_w_up, ffn_conv_w, ffn_conv_b, ffn_w_down):
    prm = dict(w_mod=w_mod, b_mod=b_mod, g_norm_mix=g_norm_mix, g_norm_ffn=g_norm_ffn,
               mla_w_dq=mla_w_dq, mla_g_qa=mla_g_qa, mla_w_uq=mla_w_uq, mla_g_qn=mla_g_qn,
               mla_w_dkv=mla_w_dkv, mla_g_kva=mla_g_kva, mla_w_ukv=mla_w_ukv, mla_g_kn=mla_g_kn,
               mla_w_o=mla_w_o, s5_a_re=s5_a_re, s5_a_im=s5_a_im, s5_log_dt=s5_log_dt,
               s5_b_re=s5_b_re, s5_b_im=s5_b_im, s5_c_re=s5_c_re, s5_c_im=s5_c_im, s5_d=s5_d,
               s5_w_glu=s5_w_glu, s5_w_gate=s5_w_gate, ffn_w_gate=ffn_w_gate, ffn_w_up=ffn_w_up,
               ffn_conv_w=ffn_conv_w, ffn_conv_b=ffn_conv_b, ffn_w_down=ffn_w_down)
    xp, xs = x_prompt, x_sample
    ckv_p, kr_p, ckv_s, kr_s = [], [], [], []
    re_p, im_p, re_s, im_s = [], [], [], []
    conv_p, conv_s = [], []
    for l in range(DEPTH):
        i = l // N_MIXERS
        if l % N_MIXERS == 0:
            past_s = (cache_mla_ckv[i], cache_mla_krope[i])
        else:
            past_s = (state_s5_re[i], state_s5_im[i])
        xp, a_p, b_p, cv_p = run_layer(l, xp, c_prompt, None, None, prm)
        xs, a_s, b_s, cv_s = run_layer(l, xs, c_sample, past_s, cache_ffn_conv[l], prm)
        if l % N_MIXERS == 0:
            ckv_p.append(a_p); kr_p.append(b_p); ckv_s.append(a_s); kr_s.append(b_s)
        else:
            re_p.append(a_p); im_p.append(b_p); re_s.append(a_s); im_s.append(b_s)
        conv_p.append(cv_p); conv_s.append(cv_s)
    new_ckv_prompt = jnp.stack(ckv_p)
    new_krope_prompt = jnp.stack(kr_p)
    new_ckv_sample = jnp.stack(ckv_s)
    new_krope_sample = jnp.stack(kr_s)
    new_s5_re_prompt = jnp.stack(re_p)
    new_s5_im_prompt = jnp.stack(im_p)
    new_s5_re_sample = jnp.stack(re_s)
    new_s5_im_sample = jnp.stack(im_s)
    new_conv_prompt = jnp.stack(conv_p)
    new_conv_sample = jnp.stack(conv_s)
    return (xp, xs, new_ckv_prompt, new_krope_prompt, new_ckv_sample, new_krope_sample, new_s5_re_prompt, new_s5_im_prompt, new_s5_re_sample, new_s5_im_sample, new_conv_prompt, new_conv_sample)
```

```python
import functools
import math

import jax
import jax.numpy as jnp
from jax import lax
from jax.experimental import pallas as pl
from jax.experimental.pallas import tpu as pltpu

F32 = jnp.float32
BF16 = jnp.bfloat16

N_HEADS = 16
QK_NOPE = 128
QK_ROPE = 64
QK_HEAD = QK_NOPE + QK_ROPE
V_HEAD = 128
CHUNK = 64
ROPE_THETA = 10000.0
SSM_GROUP = 16
SSM_STATE = 64
CONV_W = 3
EPS = 1e-6

LANES = 128
SUBLANES = 8
SCAN_T = SUBLANES
VMEM_LIMIT_BYTES = 56 * 1024 * 1024
HIGHEST = lax.Precision.HIGHEST

TS = 512
TQ = 512
TF = 512
TN = 512
S5_TM = 4096


def _params(*sem):
    return pltpu.CompilerParams(dimension_semantics=sem, vmem_limit_bytes=VMEM_LIMIT_BYTES)


def _rms(x):
    return x * lax.rsqrt(jnp.mean(x * x, axis=-1, keepdims=True) + EPS)


def _norm_mod(x, g, shift, scale):
    return (_rms(x) * g) * (1.0 + scale) + shift


def _mod_kernel(c_ref, w_ref, b_ref, o_ref):
    c = c_ref[...]
    a = (c * jax.nn.sigmoid(c)).astype(BF16)
    w = w_ref[0].astype(BF16)
    o_ref[0, 0] = jnp.dot(a, w, preferred_element_type=F32) + b_ref[0, 0]


def _modulation(c_all, w_mod, b_mod):
    depth, d, d6 = w_mod.shape
    rows = c_all.shape[0]
    tn = min(1024, d)
    per = d // tn
    b4 = b_mod.reshape(depth, 6, 1, d)
    out = pl.pallas_call(
        _mod_kernel,
        grid=(depth, d6 // tn),
        in_specs=[
            pl.BlockSpec((rows, d), lambda l, n: (0, 0)),
            pl.BlockSpec((1, d, tn), lambda l, n: (l, 0, n)),
            pl.BlockSpec((1, 1, 1, tn), lambda l, n: (l, n // per, 0, n % per)),
        ],
        out_specs=pl.BlockSpec((1, 1, rows, tn), lambda l, n: (l, n // per, 0, n % per)),
        out_shape=jax.ShapeDtypeStruct((depth, 6, rows, d), F32),
        compiler_params=_params("parallel", "parallel"),
        name="modulation",
    )(c_all, w_mod, b4)
    return out.reshape(depth, 6, rows, 1, d)


def _mod_spec(layer, which, nb, width, bmap):
    return pl.BlockSpec((None, None, nb, 1, width),
                        lambda *g: (layer, which) + bmap(*g))


def _norm_rope_store(q, g_ref, tab_ref, out_ref, nb, ts):
    lane = lax.broadcasted_iota(jnp.int32, (ts, LANES), 1)
    lo64 = lane < QK_ROPE
    g0, g1, g2 = g_ref[0:1, :], g_ref[1:2, :], g_ref[2:3, :]
    for b in range(nb):
        r0 = b * ts
        for j in range(N_HEADS // 2):
            base = 2 * QK_HEAD * j
            c0 = q[r0:r0 + ts, base:base + LANES]
            c1 = q[r0:r0 + ts, base + LANES:base + 2 * LANES]
            c2 = q[r0:r0 + ts, base + 2 * LANES:base + 3 * LANES]
            sq1 = c1 * c1
            ss_a = (jnp.sum(c0 * c0, axis=-1, keepdims=True)
                    + jnp.sum(jnp.where(lo64, sq1, 0.0), axis=-1, keepdims=True))
            ss_b = (jnp.sum(jnp.where(lo64, 0.0, sq1), axis=-1, keepdims=True)
                    + jnp.sum(c2 * c2, axis=-1, keepdims=True))
            ra = lax.rsqrt(ss_a * (1.0 / QK_HEAD) + EPS)
            rb = lax.rsqrt(ss_b * (1.0 / QK_HEAD) + EPS)
            c0 = c0 * ra * g0
            c1 = c1 * jnp.where(lo64, ra, rb) * g1
            c2 = c2 * rb * g2
            c1 = (c1 * tab_ref[0] + pltpu.roll(c1, 96, axis=1) * tab_ref[1]
                  + pltpu.roll(c1, 32, axis=1) * tab_ref[2])
            c2 = (c2 * tab_ref[3] + pltpu.roll(c2, 96, axis=1) * tab_ref[4]
                  + pltpu.roll(c2, 32, axis=1) * tab_ref[5])
            out_ref[b, 2 * j, :, 0:LANES] = c0.astype(out_ref.dtype)
            out_ref[b, 2 * j, :, LANES:QK_HEAD] = c1[:, 0:QK_ROPE].astype(out_ref.dtype)
            out_ref[b, 2 * j + 1, :, 0:LANES] = c2.astype(out_ref.dtype)
            out_ref[b, 2 * j + 1, :, LANES:QK_HEAD] = (
                pltpu.roll(c1, 64, axis=1)[:, 0:QK_ROPE].astype(out_ref.dtype))


def _rope_tables(pos):
    half = QK_ROPE // 2
    inv = ROPE_THETA ** (-jnp.arange(half, dtype=F32) / half)
    ang = pos.astype(F32)[:, None] * inv[None, :]
    cos, sin = jnp.cos(ang), jnp.sin(ang)
    n = pos.shape[0]
    one = jnp.ones((n, 64), F32)
    zero64 = jnp.zeros((n, 64), F32)
    zero32 = jnp.zeros((n, half), F32)
    cos1 = jnp.concatenate([cos, cos, one], axis=1)
    sa1 = jnp.concatenate([-sin, zero32, zero64], axis=1)
    sb1 = jnp.concatenate([zero32, sin, zero64], axis=1)
    cos2 = jnp.concatenate([one, cos, cos], axis=1)
    sa2 = jnp.concatenate([zero64, -sin, zero32], axis=1)
    sb2 = jnp.concatenate([zero64, zero32, sin], axis=1)
    return jnp.stack([cos1, sa1, sb1, cos2, sa2, sb2])


def _head_gain_table(g, scale):
    g = g.astype(F32) * scale
    return jnp.stack([g[0:128], jnp.concatenate([g[128:192], g[0:64]]),
                      jnp.concatenate([g[64:128], g[128:192]])])


def _mla_pre_kernel(x_ref, sh_ref, sc_ref, gn_ref, wd_ref, gqa_ref, wuq_ref, gq_ref,
                    tab_ref, gkva_ref, q_ref, ckv_ref, kr_ref, *, nb, ts, q_lora, kv_lora):
    d = x_ref.shape[-1]
    h = _norm_mod(x_ref[...], gn_ref[...], sh_ref[...], sc_ref[...])
    h2 = h.reshape(nb * ts, d).astype(BF16)
    a = jnp.dot(h2, wd_ref[...], preferred_element_type=F32)
    qa = a[:, 0:q_lora]
    lat = a[:, q_lora:q_lora + kv_lora]
    kr = a[:, q_lora + kv_lora:q_lora + kv_lora + QK_ROPE]
    ckv_ref[...] = (_rms(lat) * gkva_ref[...]).reshape(nb, ts, kv_lora)
    kr_ref[...] = kr.reshape(nb, ts, QK_ROPE)
    qn = (_rms(qa) * gqa_ref[...]).astype(BF16)
    q = jnp.dot(qn, wuq_ref[...], preferred_element_type=F32)
    _norm_rope_store(q, gq_ref, tab_ref, q_ref, nb, ts)


def _mla_pre(x, mod, layer, bofs, g_norm, w_dqkv, g_qa, w_uq, gq_tab, tabs, g_kva, nb, ts):
    bsz, seq, d = x.shape
    q_lora = w_uq.shape[0]
    kv_lora = g_kva.shape[-1]
    grid = (seq // ts, bsz // nb)
    bmap = lambda t, b: (b + bofs, 0, 0)
    const = lambda t, b: (0, 0)
    return pl.pallas_call(
        functools.partial(_mla_pre_kernel, nb=nb, ts=ts, q_lora=q_lora, kv_lora=kv_lora),
        grid=grid,
        in_specs=[
            pl.BlockSpec((nb, ts, d), lambda t, b: (b, t, 0)),
            _mod_spec(layer, 0, nb, d, bmap),
            _mod_spec(layer, 1, nb, d, bmap),
            pl.BlockSpec((1, d), const),
            pl.BlockSpec(w_dqkv.shape, const),
            pl.BlockSpec((1, q_lora), const),
            pl.BlockSpec(w_uq.shape, const),
            pl.BlockSpec((3, LANES), const),
            pl.BlockSpec((6, ts, LANES), lambda t, b: (0, t, 0)),
            pl.BlockSpec((1, kv_lora), const),
        ],
        out_specs=[
            pl.BlockSpec((nb, N_HEADS, ts, QK_HEAD), lambda t, b: (b, 0, t, 0)),
            pl.BlockSpec((nb, ts, kv_lora), lambda t, b: (b, t, 0)),
            pl.BlockSpec((nb, ts, QK_ROPE), lambda t, b: (b, t, 0)),
        ],
        out_shape=[
            jax.ShapeDtypeStruct((bsz, N_HEADS, seq, QK_HEAD), BF16),
            jax.ShapeDtypeStruct((bsz, seq, kv_lora), F32),
            jax.ShapeDtypeStruct((bsz, seq, QK_ROPE), F32),
        ],
        compiler_params=_params("parallel", "parallel"),
        name="mla_pre",
    )(x, mod, mod, g_norm, w_dqkv, g_qa, w_uq, gq_tab, tabs, g_kva)


def _kv_kernel(ckv_ref, kr_ref, wk_ref, wkr_ref, wv_ref, gk_ref, tab_ref, k_ref, v_ref,
               *, nb, ts):
    kv_lora = ckv_ref.shape[-1]
    ckv = ckv_ref[...].reshape(nb * ts, kv_lora).astype(BF16)
    kr = kr_ref[...].reshape(nb * ts, QK_ROPE)
    kk = jnp.concatenate([kr, kr], axis=-1)
    hi = kk.astype(BF16).astype(F32)
    lane = lax.broadcasted_iota(jnp.int32, kk.shape, 1)
    hilo = jnp.where(lane < QK_ROPE, hi, kk - hi).astype(BF16)
    k = (jnp.dot(ckv, wk_ref[...], preferred_element_type=F32)
         + jnp.dot(hilo, wkr_ref[...], preferred_element_type=F32))
    v = jnp.dot(ckv, wv_ref[...], preferred_element_type=F32).astype(v_ref.dtype)
    for b in range(nb):
        for h in range(N_HEADS):
            v_ref[b, h] = v[b * ts:(b + 1) * ts, h * V_HEAD:(h + 1) * V_HEAD]
    _norm_rope_store(k, gk_ref, tab_ref, k_ref, nb, ts)


def _kv_expand(ckv, krope, w_k, w_kr, w_v, gk_tab, tabs, nb, ts):
    bsz, seq, kv_lora = ckv.shape
    const = lambda t, b: (0, 0)
    return pl.pallas_call(
        functools.partial(_kv_kernel, nb=nb, ts=ts),
        grid=(seq // ts, bsz // nb),
        in_specs=[
            pl.BlockSpec((nb, ts, kv_lora), lambda t, b: (b, t, 0)),
            pl.BlockSpec((nb, ts, QK_ROPE), lambda t, b: (b, t, 0)),
            pl.BlockSpec(w_k.shape, const),
            pl.BlockSpec(w_kr.shape, const),
            pl.BlockSpec(w_v.shape, const),
            pl.BlockSpec((3, LANES), const),
            pl.BlockSpec((6, ts, LANES), lambda t, b: (0, t, 0)),
        ],
        out_specs=[
            pl.BlockSpec((nb, N_HEADS, ts, QK_HEAD), lambda t, b: (b, 0, t, 0)),
            pl.BlockSpec((nb, N_HEADS, ts, V_HEAD), lambda t, b: (b, 0, t, 0)),
        ],
        out_shape=[
            jax.ShapeDtypeStruct((bsz, N_HEADS, seq, QK_HEAD), BF16),
            jax.ShapeDtypeStruct((bsz, N_HEADS, seq, V_HEAD), BF16),
        ],
        compiler_params=_params("parallel", "parallel"),
        name="kv_expand",
    )(ckv, krope, w_k, w_kr, w_v, gk_tab, tabs)


def _attn_step(q, k, v, mask, m_scr, l_scr, acc_scr):
    s = lax.dot_general(q, k, (((1,), (1,)), ((), ())), preferred_element_type=F32)
    if mask is not None:
        s = jnp.where(mask, s, -jnp.inf)
    m_prev = m_scr[...]
    m_new = jnp.maximum(m_prev, jnp.max(s, axis=-1, keepdims=True))
    alpha = jnp.exp(m_prev - m_new)
    p = jnp.exp(s - m_new)
    l_scr[...] = alpha * l_scr[...] + jnp.sum(p, axis=-1, keepdims=True)
    acc_scr[...] = alpha * acc_scr[...] + jnp.dot(p.astype(v.dtype), v,
                                                  preferred_element_type=F32)
    m_scr[...] = m_new


def _attn_init(m_scr, l_scr, acc_scr):
    m_scr[...] = jnp.full(m_scr.shape, -jnp.inf, F32)
    l_scr[...] = jnp.zeros(l_scr.shape, F32)
    acc_scr[...] = jnp.zeros(acc_scr.shape, F32)


def _attn_causal_kernel(q_ref, k_ref, v_ref, o_ref, m_scr, l_scr, acc_scr, *, tq):
    i = pl.program_id(2)
    q = q_ref[0, 0]
    _attn_init(m_scr, l_scr, acc_scr)

    def body(j, carry):
        off = pl.multiple_of(j * tq, tq)
        _attn_step(q, k_ref[0, 0, pl.ds(off, tq), :], v_ref[0, 0, pl.ds(off, tq), :],
                   None, m_scr, l_scr, acc_scr)
        return carry

    lax.fori_loop(0, i, body, 0)
    off = pl.multiple_of(i * tq, tq)
    rq = lax.broadcasted_iota(jnp.int32, (tq, tq), 0) // CHUNK
    ck = lax.broadcasted_iota(jnp.int32, (tq, tq), 1) // CHUNK
    _attn_step(q, k_ref[0, 0, pl.ds(off, tq), :], v_ref[0, 0, pl.ds(off, tq), :],
               ck <= rq, m_scr, l_scr, acc_scr)
    o_ref[0] = (acc_scr[...] / l_scr[...]).astype(o_ref.dtype)


def _attn_causal(q, k, v, tq):
    bsz, nh, seq, _ = q.shape
    return pl.pallas_call(
        functools.partial(_attn_causal_kernel, tq=tq),
        grid=(bsz, nh, seq // tq),
        in_specs=[
            pl.BlockSpec((1, 1, tq, QK_HEAD), lambda b, h, i: (b, h, i, 0)),
            pl.BlockSpec((1, 1, seq, QK_HEAD), lambda b, h, i: (b, h, 0, 0)),
            pl.BlockSpec((1, 1, seq, V_HEAD), lambda b, h, i: (b, h, 0, 0)),
        ],
        out_specs=pl.BlockSpec((1, tq, V_HEAD), lambda b, h, i: (b, i, h)),
        out_shape=jax.ShapeDtypeStruct((bsz, seq, nh * V_HEAD), BF16),
        scratch_shapes=[pltpu.VMEM((tq, 1), F32), pltpu.VMEM((tq, 1), F32),
                        pltpu.VMEM((tq, V_HEAD), F32)],
        compiler_params=_params("parallel", "parallel", "parallel"),
        name="attn_causal",
    )(q, k, v)


def _attn_cached_kernel(q_ref, kc_ref, vc_ref, kn_ref, vn_ref, o_ref, m_scr, l_scr, acc_scr,
                        *, tk, n_tiles):
    q = q_ref[0, 0]
    _attn_init(m_scr, l_scr, acc_scr)

    def body(j, carry):
        off = pl.multiple_of(j * tk, tk)
        _attn_step(q, kc_ref[0, 0, pl.ds(off, tk), :], vc_ref[0, 0, pl.ds(off, tk), :],
                   None, m_scr, l_scr, acc_scr)
        return carry

    lax.fori_loop(0, n_tiles, body, 0)
    _attn_step(q, kn_ref[0, 0], vn_ref[0, 0], None, m_scr, l_scr, acc_scr)
    o_ref[0] = (acc_scr[...] / l_scr[...]).astype(o_ref.dtype)


def _attn_cached(q, kc, vc, kn, vn, tk):
    bsz, nh, seq, _ = q.shape
    past = kc.shape[2]
    blk = lambda n, w: pl.BlockSpec((1, 1, n, w), lambda b, h: (b, h, 0, 0))
    return pl.pallas_call(
        functools.partial(_attn_cached_kernel, tk=tk, n_tiles=past // tk),
        grid=(bsz, nh),
        in_specs=[blk(seq, QK_HEAD), blk(past, QK_HEAD), blk(past, V_HEAD),
                  blk(seq, QK_HEAD), blk(seq, V_HEAD)],
        out_specs=pl.BlockSpec((1, seq, V_HEAD), lambda b, h: (b, 0, h)),
        out_shape=jax.ShapeDtypeStruct((bsz, seq, nh * V_HEAD), BF16),
        scratch_shapes=[pltpu.VMEM((seq, 1), F32), pltpu.VMEM((seq, 1), F32),
                        pltpu.VMEM((seq, V_HEAD), F32)],
        compiler_params=_params("parallel", "parallel"),
        name="attn_cached",
    )(q, kc, vc, kn, vn)


def _oproj_kernel(a_ref, w_ref, x_ref, gt_ref, o_ref, *, nb, ts):
    a = a_ref[...].reshape(nb * ts, a_ref.shape[-1])
    y = jnp.dot(a, w_ref[...], preferred_element_type=F32)
    o_ref[...] = x_ref[...] + gt_ref[...] * y.reshape(nb, ts, y.shape[-1])


def _glu_kernel(z_ref, wa_ref, wb_ref, x_ref, gt_ref, o_ref, *, nb, ts):
    z = z_ref[...].reshape(nb * ts, z_ref.shape[-1])
    a = jnp.dot(z, wa_ref[...], preferred_element_type=F32)
    b = jnp.dot(z, wb_ref[...], preferred_element_type=F32)
    y = a * jax.nn.sigmoid(b)
    o_ref[...] = x_ref[...] + gt_ref[...] * y.reshape(nb, ts, y.shape[-1])


def _gated_proj(kernel_fn, name, a, weights, x, mod, layer, bofs, nb, ts, tn):
    bsz, seq, d = x.shape
    kdim = a.shape[-1]
    grid = (d // tn, seq // ts, bsz // nb)
    wspecs = [pl.BlockSpec((kdim, tn), lambda n, t, b: (0, n)) for _ in weights]
    return pl.pallas_call(
        functools.partial(kernel_fn, nb=nb, ts=ts),
        grid=grid,
        in_specs=[pl.BlockSpec((nb, ts, kdim), lambda n, t, b: (b, t, 0))] + wspecs + [
            pl.BlockSpec((nb, ts, tn), lambda n, t, b: (b, t, n)),
            _mod_spec(layer, 2, nb, tn, lambda n, t, b: (b + bofs, 0, n)),
        ],
        out_specs=pl.BlockSpec((nb, ts, tn), lambda n, t, b: (b, t, n)),
        out_shape=jax.ShapeDtypeStruct(x.shape, F32),
        compiler_params=_params("parallel", "parallel", "parallel"),
        name=name,
    )(a, *weights, x, mod)


def _ffn_kernel(x_ref, sh_ref, sc_ref, gt_ref, gn_ref, wg_ref, wu_ref, cw_ref, cb_ref,
                wd_ref, past_ref, o_ref, conv_ref, h_scr, acc_scr, g_scr, carry_scr,
                *, nb, ts):
    t = pl.program_id(1)
    f = pl.program_id(2)
    nf = pl.num_programs(2)
    d = x_ref.shape[-1]
    tf = wg_ref.shape[-1]
    halo = SUBLANES

    @pl.when(f == 0)
    def _():
        h = _norm_mod(x_ref[...], gn_ref[...], sh_ref[...], sc_ref[...])
        h_scr[...] = h.reshape(nb * ts, d).astype(BF16)
        acc_scr[...] = jnp.zeros(acc_scr.shape, F32)

    @pl.when(t == 0)
    def _():
        carry_scr[f, :, halo - (CONV_W - 1):halo, :] = past_ref[...]

    h = h_scr[...]
    g = jnp.dot(h, wg_ref[...], preferred_element_type=F32)
    u = jnp.dot(h, wu_ref[...], preferred_element_type=F32)
    g_scr[:, halo - (CONV_W - 1):halo, :] = carry_scr[f, :, halo - (CONV_W - 1):halo, :]
    g_scr[:, halo:halo + ts, :] = g.reshape(nb, ts, tf)
    carry_scr[f, :, halo - (CONV_W - 1):halo, :] = g_scr[:, halo + ts - (CONV_W - 1):halo + ts, :]
    conv_ref[...] = g_scr[:, halo + ts - (CONV_W - 1):halo + ts, :]
    gc = cb_ref[...] + cw_ref[0:1, :] * g_scr[:, halo - 2:halo - 2 + ts, :]
    gc = gc + cw_ref[1:2, :] * g_scr[:, halo - 1:halo - 1 + ts, :]
    gc = gc + cw_ref[2:3, :] * g_scr[:, halo:halo + ts, :]
    act = (gc * jax.nn.sigmoid(gc)).reshape(nb * ts, tf) * u
    acc_scr[...] += jnp.dot(act.astype(BF16), wd_ref[...], preferred_element_type=F32)

    @pl.when(f == nf - 1)
    def _():
        o_ref[...] = x_ref[...] + gt_ref[...] * acc_scr[...].reshape(nb, ts, d)


def _conv_ffn(x, mod, layer, bofs, g_norm, w_gate, w_up, conv_w, conv_b, w_down, past,
              nb, ts, tf):
    bsz, seq, d = x.shape
    dff = w_gate.shape[-1]
    nf = dff // tf
    grid = (bsz // nb, seq // ts, nf)
    bmap = lambda b, t, f: (b + bofs, 0, 0)
    y, conv = pl.pallas_call(
        functools.partial(_ffn_kernel, nb=nb, ts=ts),
        grid=grid,
        in_specs=[
            pl.BlockSpec((nb, ts, d), lambda b, t, f: (b, t, 0)),
            _mod_spec(layer, 3, nb, d, bmap),
            _mod_spec(layer, 4, nb, d, bmap),
            _mod_spec(layer, 5, nb, d, bmap),
            pl.BlockSpec((1, d), lambda b, t, f: (0, 0)),
            pl.BlockSpec((d, tf), lambda b, t, f: (0, f)),
            pl.BlockSpec((d, tf), lambda b, t, f: (0, f)),
            pl.BlockSpec((CONV_W, tf), lambda b, t, f: (0, f)),
            pl.BlockSpec((1, tf), lambda b, t, f: (0, f)),
            pl.BlockSpec((tf, d), lambda b, t, f: (f, 0)),
            pl.BlockSpec((nb, CONV_W - 1, tf), lambda b, t, f: (b, 0, f)),
        ],
        out_specs=[
            pl.BlockSpec((nb, ts, d), lambda b, t, f: (b, t, 0)),
            pl.BlockSpec((nb, None, CONV_W - 1, tf), lambda b, t, f: (b, t, 0, f)),
        ],
        out_shape=[
            jax.ShapeDtypeStruct(x.shape, F32),
            jax.ShapeDtypeStruct((bsz, seq // ts, CONV_W - 1, dff), F32),
        ],
        scratch_shapes=[
            pltpu.VMEM((nb * ts, d), BF16),
            pltpu.VMEM((nb * ts, d), F32),
            pltpu.VMEM((nb, ts + SUBLANES, tf), F32),
            pltpu.VMEM((nf, nb, SUBLANES, tf), F32),
        ],
        compiler_params=_params("arbitrary", "arbitrary", "arbitrary"),
        name="conv_ffn",
    )(x, mod, mod, mod, g_norm, w_gate, w_up, conv_w, conv_b, w_down, past)
    return y, conv[:, -1]


def _s5_pre_kernel(x_ref, sh_ref, sc_ref, gn_ref, u_ref):
    u_ref[...] = _norm_mod(x_ref[...], gn_ref[...], sh_ref[...], sc_ref[...])


def _s5_pre(x, mod, layer, bofs, g_norm, nb, ts):
    bsz, seq, d = x.shape
    bmap = lambda b, t: (b + bofs, 0, 0)
    return pl.pallas_call(
        _s5_pre_kernel,
        grid=(bsz // nb, seq // ts),
        in_specs=[
            pl.BlockSpec((nb, ts, d), lambda b, t: (b, t, 0)),
            _mod_spec(layer, 0, nb, d, bmap),
            _mod_spec(layer, 1, nb, d, bmap),
            pl.BlockSpec((1, d), lambda b, t: (0, 0)),
        ],
        out_specs=pl.BlockSpec((nb, ts, d), lambda b, t: (b, t, 0)),
        out_shape=jax.ShapeDtypeStruct(x.shape, F32),
        compiler_params=_params("parallel", "parallel"),
        name="s5_pre",
    )(x, mod, mod, g_norm)


def _s5_kernel(u_ref, mi_ref, win_ref, wout_ref, a_ref, d_ref, h0_ref, z_ref, hT_ref,
               st_scr, s_scr, hs_scr, zf_scr, *, nseq, tm):
    t_idx = pl.program_id(2)
    tc = tm // SCAN_T
    nk = st_scr.shape[-1] // LANES
    hk = nk // 2
    col = lambda k: slice(k * LANES, (k + 1) * LANES)

    @pl.when(t_idx == 0)
    def _():
        st_scr[...] = h0_ref[0, 0]

    us = [u_ref[:, pl.ds(s, tc, stride=SCAN_T), :].reshape(nseq * tc, LANES)
          for s in range(SCAN_T)]
    z = jnp.concatenate([v.astype(BF16) for v in us], axis=-1)
    inj = jnp.dot(z, win_ref[0], preferred_element_type=F32)
    for k in range(nk):
        s_scr[k] = inj[:, col(k)]
    y = jnp.dot(z, mi_ref[0], preferred_element_type=F32)

    a = [a_ref[0, 0:1, col(k)] for k in range(nk)]

    def advance(c, st, rows):
        for k in range(nk):
            hs_scr[k, rows, :] = st[k]
        re = [a[k] * st[k] - a[k + hk] * st[k + hk] + s_scr[k, rows, :] for k in range(hk)]
        im = [a[k] * st[k + hk] + a[k + hk] * st[k] + s_scr[k + hk, rows, :] for k in range(hk)]
        return tuple(re + im)

    st = tuple(st_scr[:, col(k)] for k in range(nk))
    if nseq == 1:
        st = lax.fori_loop(0, tc, lambda c, s: advance(c, s, pl.ds(c, 1)), st, unroll=8)
    else:
        for c in range(tc):
            st = advance(c, st, pl.ds(c, nseq, stride=tc))
    for k in range(nk):
        st_scr[:, col(k)] = st[k]
    hT_ref[0, 0] = st_scr[...]

    hs = jnp.concatenate([hs_scr[k] for k in range(nk)], axis=-1)
    y = y + jnp.dot(hs.astype(BF16), wout_ref[0], preferred_element_type=F32)
    dvec = d_ref[0]
    for s in range(SCAN_T):
        val = y[:, s * LANES:(s + 1) * LANES] + dvec * us[s]
        zf_scr[:, pl.ds(s, tc, stride=SCAN_T), :] = jax.nn.gelu(val).reshape(nseq, tc, LANES)
    z_ref[...] = zf_scr[...].astype(z_ref.dtype)


def _s5_layer(u, mats, d_vec, h0, nseq, tm):
    bsz, seq, d = u.shape
    m_intra, w_in, w_out, a_pow = mats
    nblk = d // LANES
    width = w_in.shape[-1]
    tc = tm // SCAN_T
    grid = (nblk, bsz // nseq, seq // tm)
    wspec = lambda arr: pl.BlockSpec((1,) + arr.shape[1:], lambda j, b, t: (j, 0, 0))
    z, h_last = pl.pallas_call(
        functools.partial(_s5_kernel, nseq=nseq, tm=tm),
        grid=grid,
        in_specs=[
            pl.BlockSpec((nseq, tm, LANES), lambda j, b, t: (b, t, j)),
            wspec(m_intra), wspec(w_in), wspec(w_out), wspec(a_pow),
            pl.BlockSpec((1, 1, LANES), lambda j, b, t: (j, 0, 0)),
            pl.BlockSpec((1, 1, nseq, width), lambda j, b, t: (j, b, 0, 0)),
        ],
        out_specs=[
            pl.BlockSpec((nseq, tm, LANES), lambda j, b, t: (b, t, j)),
            pl.BlockSpec((1, 1, nseq, width), lambda j, b, t: (j, b, 0, 0)),
        ],
        out_shape=[
            jax.ShapeDtypeStruct((bsz, seq, d), BF16),
            jax.ShapeDtypeStruct((nblk, bsz // nseq, nseq, width), F32),
        ],
        scratch_shapes=[
            pltpu.VMEM((nseq, width), F32),
            pltpu.VMEM((width // LANES, nseq * tc, LANES), F32),
            pltpu.VMEM((width // LANES, nseq * tc, LANES), F32),
            pltpu.VMEM((nseq, tm, LANES), F32),
        ],
        compiler_params=_params("arbitrary", "arbitrary", "arbitrary"),
        name="s5_scan",
    )(u, m_intra, w_in, w_out, a_pow, d_vec, h0.reshape(nblk, bsz // nseq, nseq, width))
    return z, h_last.reshape(nblk, bsz, width)


def _s5_operators(a_re, a_im, log_dt, b_re, b_im, c_re, c_im):
    n_groups, n_state = a_re.shape
    p_dim = b_re.shape[-1]
    gpb = LANES // p_dim
    nblk = n_groups // gpb
    t_len = SCAN_T
    dt = jnp.exp(log_dt.astype(F32))[:, None]

    def powers(k):
        kk = k.astype(F32)[:, None, None]
        mag = jnp.exp(a_re * dt * kk)
        ang = a_im * dt * kk
        return mag * jnp.cos(ang), mag * jnp.sin(ang)

    steps = jnp.arange(t_len + 1)
    pw_re, pw_im = powers(steps)
    num_re, num_im = pw_re[1] - 1.0, pw_im[1]
    den = a_re * a_re + a_im * a_im
    f_re = (num_re * a_re + num_im * a_im) / den
    f_im = (num_im * a_re - num_re * a_im) / den
    bb_re = f_re[..., None] * b_re - f_im[..., None] * b_im
    bb_im = f_re[..., None] * b_im + f_im[..., None] * b_re
    eye = jnp.eye(gpb, dtype=F32)

    rp_re, rp_im = powers(t_len - 1 - steps[:t_len])
    wi_re = rp_re[..., None] * bb_re[None] - rp_im[..., None] * bb_im[None]
    wi_im = rp_re[..., None] * bb_im[None] + rp_im[..., None] * bb_re[None]
    wi = jnp.stack([wi_re, wi_im])
    wi = wi.reshape(2, t_len, nblk, gpb, n_state, p_dim)
    w_in = jnp.einsum('rsjgnq,gh->jsgqrhn', wi, eye)
    w_in = w_in.reshape(nblk, t_len * gpb * p_dim, 2 * gpb * n_state)

    fp_re, fp_im = pw_re[1:], pw_im[1:]
    ca_re = c_re[None] * fp_re[:, :, None, :] - c_im[None] * fp_im[:, :, None, :]
    ca_im = c_re[None] * fp_im[:, :, None, :] + c_im[None] * fp_re[:, :, None, :]
    wo = jnp.stack([ca_re, -ca_im]).reshape(2, t_len, nblk, gpb, p_dim, n_state)
    w_out = jnp.einsum('rtjgpn,gh->jrgnthp', wo, eye)
    w_out = w_out.reshape(nblk, 2 * gpb * n_state, t_len * gpb * p_dim)

    cd_re = c_re[None] * pw_re[:t_len, :, None, :] - c_im[None] * pw_im[:t_len, :, None, :]
    cd_im = c_re[None] * pw_im[:t_len, :, None, :] + c_im[None] * pw_re[:t_len, :, None, :]
    taps = (jnp.einsum('dgpn,gnq->gdpq', cd_re, bb_re, precision=HIGHEST)
            - jnp.einsum('dgpn,gnq->gdpq', cd_im, bb_im, precision=HIGHEST))
    s_i = jnp.arange(t_len)[:, None]
    t_i = jnp.arange(t_len)[None, :]
    toe = taps[:, jnp.clip(t_i - s_i, 0, t_len - 1)]
    toe = jnp.where((t_i >= s_i)[None, :, :, None, None], toe, 0.0)
    toe = toe.reshape(nblk, gpb, t_len, t_len, p_dim, p_dim)
    m_intra = jnp.einsum('jgstpq,gh->jsgqthp', toe, eye)
    m_intra = m_intra.reshape(nblk, t_len * gpb * p_dim, t_len * gpb * p_dim)

    a_pow = jnp.concatenate([pw_re[t_len].reshape(nblk, 1, gpb * n_state),
                             pw_im[t_len].reshape(nblk, 1, gpb * n_state)], axis=-1)
    return m_intra.astype(BF16), w_in.astype(BF16), w_out.astype(BF16), a_pow


def _state_to_blocks(re, im, nblk):
    bsz = re.shape[0]
    pack = lambda a: a.astype(F32).reshape(bsz, nblk, -1).transpose(1, 0, 2)
    return jnp.concatenate([pack(re), pack(im)], axis=-1)


def _blocks_to_state(h, n_groups, n_state):
    nblk, bsz, width = h.shape
    half = width // 2
    unpack = lambda a: a.transpose(1, 0, 2).reshape(bsz, n_groups, n_state)
    return unpack(h[..., :half]), unpack(h[..., half:])


def _tile(n, pref):
    return pref if n % pref == 0 else n


def kernel(x_prompt, x_sample, cache_mla_ckv, cache_mla_krope, state_s5_re, state_s5_im, cache_ffn_conv, c_prompt, c_sample, w_mod, b_mod, g_norm_mix, g_norm_ffn, mla_w_dq, mla_g_qa, mla_w_uq, mla_g_qn, mla_w_dkv, mla_g_kva, mla_w_ukv, mla_g_kn, mla_w_o, s5_a_re, s5_a_im, s5_log_dt, s5_b_re, s5_b_im, s5_c_re, s5_c_im, s5_d, s5_w_glu, s5_w_gate, ffn_w_gate, ffn_w_up, ffn_conv_w, ffn_conv_b, ffn_w_down):
    depth = w_mod.shape[0]
    bp, lp, d = x_prompt.shape
    bs, ls, _ = x_sample.shape
    past = cache_mla_ckv.shape[2]
    kv_lora = cache_mla_ckv.shape[3]
    dff = ffn_w_gate.shape[-1]
    n_groups, n_state = s5_a_re.shape[1], s5_a_re.shape[2]
    nblk = d // LANES

    s_ofs = -(-bp // bs) * bs
    c_all = jnp.zeros((s_ofs + bs, d), F32).at[:bp].set(c_prompt).at[s_ofs:].set(c_sample)
    mod = _modulation(c_all, w_mod, b_mod)
    s_blk = s_ofs // bs

    assert past % CHUNK == 0 and ls <= CHUNK
    ts_p = _tile(lp, TS)
    tq_p = _tile(lp, TQ)
    tf = _tile(dff, TF)
    tn = _tile(d, TN)
    tk_c = _tile(past, TQ)
    ts_c = _tile(past, TS)
    tm_p = _tile(lp, S5_TM)

    tabs_p = _rope_tables(jnp.arange(lp, dtype=jnp.int32))
    tabs_c = _rope_tables(jnp.arange(past, dtype=jnp.int32))
    tabs_s = _rope_tables(past + jnp.arange(ls, dtype=jnp.int32))
    zero_conv = jnp.zeros((bp, CONV_W - 1, dff), F32)
    zero_state = jnp.zeros((nblk, bp, 2 * (n_groups // nblk) * n_state), F32)

    xp, xs = x_prompt, x_sample
    outs = {k: [] for k in ("ckv_p", "kr_p", "ckv_s", "kr_s", "re_p", "im_p", "re_s", "im_s",
                            "conv_p", "conv_s")}
    for l in range(depth):
        i = l // 2
        g_mix = g_norm_mix[l][None, :]
        g_ffn = g_norm_ffn[l][None, :]
        if l % 2 == 0:
            w_dqkv = jnp.concatenate([mla_w_dq[i], mla_w_dkv[i]], axis=1).astype(BF16)
            w_uq = mla_w_uq[i].astype(BF16)
            gq_tab = _head_gain_table(mla_g_qn[i], QK_HEAD ** -0.5)
            gk_tab = _head_gain_table(mla_g_kn[i], 1.0)
            g_qa = mla_g_qa[i][None, :]
            g_kva = mla_g_kva[i][None, :]
            ukv = mla_w_ukv[i].reshape(kv_lora, N_HEADS, QK_NOPE + V_HEAD)
            w_k = jnp.concatenate(
                [ukv[:, :, :QK_NOPE], jnp.zeros((kv_lora, N_HEADS, QK_ROPE), F32)],
                axis=-1).reshape(kv_lora, N_HEADS * QK_HEAD).astype(BF16)
            w_v = ukv[:, :, QK_NOPE:].reshape(kv_lora, N_HEADS * V_HEAD).astype(BF16)
            eye_r = jnp.eye(QK_ROPE, dtype=F32)
            scatter = jnp.concatenate(
                [jnp.zeros((QK_ROPE, N_HEADS, QK_NOPE), F32),
                 jnp.broadcast_to(eye_r[:, None, :], (QK_ROPE, N_HEADS, QK_ROPE))],
                axis=-1).reshape(QK_ROPE, N_HEADS * QK_HEAD)
            w_kr = jnp.concatenate([scatter, scatter], axis=0).astype(BF16)
            w_o = mla_w_o[i].astype(BF16)

            q, ckv, kr = _mla_pre(xp, mod, l, 0, g_mix, w_dqkv, g_qa, w_uq, gq_tab, tabs_p,
                                  g_kva, 1, ts_p)
            k, v = _kv_expand(ckv, kr, w_k, w_kr, w_v, gk_tab, tabs_p, 1, ts_p)
            o = _attn_causal(q, k, v, tq_p)
            xp = _gated_proj(_oproj_kernel, "mla_out", o, [w_o], xp, mod, l, 0, 1, ts_p, tn)
            outs["ckv_p"].append(ckv)
            outs["kr_p"].append(kr)
            q, ckv, kr = _mla_pre(xs, mod, l, s_blk, g_mix, w_dqkv, g_qa, w_uq, gq_tab, tabs_s,
                                  g_kva, bs, ls)
            kc, vc = _kv_expand(cache_mla_ckv[i], cache_mla_krope[i], w_k, w_kr, w_v, gk_tab,
                                tabs_c, 1, ts_c)
            kn, vn = _kv_expand(ckv, kr, w_k, w_kr, w_v, gk_tab, tabs_s, bs, ls)
            o = _attn_cached(q, kc, vc, kn, vn, tk_c)
            xs = _gated_proj(_oproj_kernel, "mla_out", o, [w_o], xs, mod, l, s_blk, bs, ls, tn)
            outs["ckv_s"].append(ckv)
            outs["kr_s"].append(kr)
        else:
            mats = _s5_operators(s5_a_re[i], s5_a_im[i], s5_log_dt[i], s5_b_re[i], s5_b_im[i],
                                 s5_c_re[i], s5_c_im[i])
            d_vec = s5_d[i].astype(F32).reshape(nblk, 1, LANES)
            w_glu = s5_w_glu[i].astype(BF16)
            w_gate = s5_w_gate[i].astype(BF16)
            u = _s5_pre(xp, mod, l, 0, g_mix, 1, ts_p)
            z, h_last = _s5_layer(u, mats, d_vec, zero_state, 1, tm_p)
            xp = _gated_proj(_glu_kernel, "s5_glu", z, [w_glu, w_gate], xp, mod, l, 0, 1, ts_p, tn)
            re, im = _blocks_to_state(h_last, n_groups, n_state)
            outs["re_p"].append(re)
            outs["im_p"].append(im)
            h0 = _state_to_blocks(state_s5_re[i], state_s5_im[i], nblk)
            u = _s5_pre(xs, mod, l, s_blk, g_mix, bs, ls)
            z, h_last = _s5_layer(u, mats, d_vec, h0, bs, ls)
            xs = _gated_proj(_glu_kernel, "s5_glu", z, [w_glu, w_gate], xs, mod, l, s_blk, bs, ls, tn)
            re, im = _blocks_to_state(h_last, n_groups, n_state)
            outs["re_s"].append(re)
            outs["im_s"].append(im)

        wg = ffn_w_gate[l].astype(BF16)
        wu = ffn_w_up[l].astype(BF16)
        wd = ffn_w_down[l].astype(BF16)
        cw = ffn_conv_w[l].astype(F32)
        cb = ffn_conv_b[l].astype(F32)[None, :]
        xp, cv = _conv_ffn(xp, mod, l, 0, g_ffn, wg, wu, cw, cb, wd, zero_conv, 1, ts_p, tf)
        outs["conv_p"].append(cv)
        xs, cv = _conv_ffn(xs, mod, l, s_blk, g_ffn, wg, wu, cw, cb, wd, cache_ffn_conv[l],
                           bs, ls, tf)
        outs["conv_s"].append(cv)

    st = lambda name: jnp.stack(outs[name])
    return (xp, xs, st("ckv_p"), st("kr_p"), st("ckv_s"), st("kr_s"), st("re_p"), st("im_p"),
            st("re_s"), st("im_s"), st("conv_p"), st("conv_s"))
```
